```python
import math, functools
import jax, jax.numpy as jnp
from jax import lax
import numpy as np

D_MODEL = 1024
BATCH = 8
SEQ = 2048
DEPTH = 1
DEC_BATCH = 32
DEC_SEQ = 1
PAST_LEN = 8192
PAGE_SIZE = 128

ATT_HEADS = 8
HEAD_DIM = 64
ATT_WIDTH = ATT_HEADS * HEAD_DIM
Q_BLOCK = 128
SSM_GROUP = 16
SSM_WIDTH = 512
SSM_GROUPS = SSM_WIDTH // SSM_GROUP
SSM_STATE = 64
SSM_CHUNK = 128
DT_MIN = 0.001
DT_MAX = 0.1
PEER_HEADS = 8
N_KEYS = 128
N_EXPERTS = N_KEYS * N_KEYS
PEER_TOPK = 16
PEER_KEY_DIM = 256
PEER_HALF = PEER_KEY_DIM // 2
PEER_BLOCK = 128
PLE_DIM = 256
RMS_EPS = 1e-6

IN_SPLITS = (ATT_WIDTH, 2 * ATT_WIDTH, 3 * ATT_WIDTH, 3 * ATT_WIDTH + ATT_HEADS,
             3 * ATT_WIDTH + ATT_HEADS + SSM_WIDTH, 3 * ATT_WIDTH + ATT_HEADS + SSM_WIDTH + D_MODEL)
IN_WIDTH = 3 * ATT_WIDTH + ATT_HEADS + SSM_WIDTH + 2 * D_MODEL

kernel_name = 'fox_s5_peer_hybrid_step'


def rms_norm(x, g):
    x32 = x.astype(jnp.float32)
    y = x32 * lax.rsqrt(jnp.mean(x32 * x32, axis=-1, keepdims=True) + RMS_EPS)
    return (y * g.astype(jnp.float32)).astype(x.dtype)


def ssm_discretise(lam_re, lam_im, log_dt, b_re, b_im):
    f32 = jnp.float32
    dt = jnp.exp(log_dt.astype(f32))[:, None]
    lr, li = lam_re.astype(f32), lam_im.astype(f32)
    mag = jnp.exp(lr * dt)
    ang = li * dt
    a_re, a_im = mag * jnp.cos(ang), mag * jnp.sin(ang)
    den = lr * lr + li * li
    n_re, n_im = a_re - 1.0, a_im
    c_re = (n_re * lr + n_im * li) / den
    c_im = (n_im * lr - n_re * li) / den
    br, bi = b_re.astype(f32), b_im.astype(f32)
    bb_re = c_re[..., None] * br - c_im[..., None] * bi
    bb_im = c_re[..., None] * bi + c_im[..., None] * br
    return a_re, a_im, bb_re, bb_im


def _cplx_combine(e1, e2):
    a1r, a1i, b1r, b1i = e1
    a2r, a2i, b2r, b2i = e2
    return (a2r * a1r - a2i * a1i, a2r * a1i + a2i * a1r,
            a2r * b1r - a2i * b1i + b2r, a2r * b1i + a2i * b1r + b2i)


def ssm_scan(u, h0_re, h0_im, a_re, a_im, bb_re, bb_im, c_re, c_im):
    bsz, L = u.shape[0], u.shape[1]
    lc = SSM_CHUNK if L % SSM_CHUNK == 0 else L
    nc = L // lc
    u_chunks = u.reshape(bsz, nc, lc, SSM_GROUPS, SSM_GROUP).swapaxes(0, 1)
    cr, ci = c_re.astype(jnp.float32), c_im.astype(jnp.float32)

    def step(carry, uc):
        hr, hi = carry
        br = jnp.einsum('blgc,gnc->blgn', uc, bb_re)
        bi = jnp.einsum('blgc,gnc->blgn', uc, bb_im)
        ar = jnp.broadcast_to(a_re, br.shape)
        ai = jnp.broadcast_to(a_im, br.shape)
        pr, pi, sr, si = lax.associative_scan(_cplx_combine, (ar, ai, br, bi), axis=1)
        xr = pr * hr[:, None] - pi * hi[:, None] + sr
        xi = pr * hi[:, None] + pi * hr[:, None] + si
        y = jnp.einsum('blgn,gcn->blgc', xr, cr) - jnp.einsum('blgn,gcn->blgc', xi, ci)
        return (xr[:, -1], xi[:, -1]), y

    (hr, hi), ys = lax.scan(step, (h0_re, h0_im), u_chunks)
    y = ys.swapaxes(0, 1).reshape(bsz, L, SSM_GROUPS, SSM_GROUP)
    return y, hr, hi


def fox_prompt(q, k, v, logf):
    bsz, L = q.shape[0], q.shape[1]
    scale = HEAD_DIM ** -0.5
    c = jnp.cumsum(logf, axis=1)
    c_t = c.transpose(0, 2, 1)
    nb = L // Q_BLOCK
    qb = q.reshape(bsz, nb, Q_BLOCK, ATT_HEADS, HEAD_DIM).swapaxes(0, 1)
    cb = c.reshape(bsz, nb, Q_BLOCK, ATT_HEADS).swapaxes(0, 1)
    kpos = jnp.arange(L)

    def block(args):
        qi, ci, bi = args
        s = jnp.einsum('bqhd,bkhd->bhqk', qi, k).astype(jnp.float32) * scale
        s = s + ci.transpose(0, 2, 1)[..., None] - c_t[:, :, None, :]
        qpos = bi * Q_BLOCK + jnp.arange(Q_BLOCK)
        s = jnp.where(kpos[None, :] <= qpos[:, None], s, -jnp.inf)
        p = jax.nn.softmax(s, axis=-1).astype(v.dtype)
        return jnp.einsum('bhqk,bkhd->bqhd', p, v)

    o = lax.map(block, (qb, cb, jnp.arange(nb)))
    return o.swapaxes(0, 1).reshape(bsz, L, ATT_WIDTH)


def fox_sample(q, k, v, logf, k_past, v_past, logf_past):
    bsz, T = q.shape[0], q.shape[1]
    P = k_past.shape[1]
    scale = HEAD_DIM ** -0.5
    k_all = jnp.concatenate([k_past.astype(k.dtype), k], axis=1)
    v_all = jnp.concatenate([v_past.astype(v.dtype), v], axis=1)
    c = jnp.cumsum(jnp.concatenate([logf_past.astype(jnp.float32), logf], axis=1), axis=1)
    c_t = c.transpose(0, 2, 1)
    s = jnp.einsum('bqhd,bkhd->bhqk', q, k_all).astype(jnp.float32) * scale
    s = s + c_t[:, :, P:, None] - c_t[:, :, None, :]
    qpos = P + jnp.arange(T)
    kpos = jnp.arange(P + T)
    s = jnp.where(kpos[None, :] <= qpos[:, None], s, -jnp.inf)
    p = jax.nn.softmax(s, axis=-1).astype(v.dtype)
    return jnp.einsum('bhqk,bkhd->bqhd', p, v_all).reshape(bsz, T, ATT_WIDTH)


def peer(x, w_q, keys, u_tab, v_tab):
    T = x.shape[0]
    blk = min(PEER_BLOCK, T)
    Tp = -(-T // blk) * blk
    xp = jnp.pad(x, ((0, Tp - T), (0, 0)))

    def block(xb):
        q = (xb @ w_q).reshape(blk, PEER_HEADS, 2, PEER_HALF)
        s = jnp.einsum('thpd,hpkd->thpk', q, keys).astype(jnp.float32)
        s1, i1 = lax.top_k(s[:, :, 0], PEER_TOPK)
        s2, i2 = lax.top_k(s[:, :, 1], PEER_TOPK)
        cand = (s1[..., :, None] + s2[..., None, :]).reshape(blk, PEER_HEADS, PEER_TOPK * PEER_TOPK)
        cidx = (i1[..., :, None] * N_KEYS + i2[..., None, :]).reshape(blk, PEER_HEADS, PEER_TOPK * PEER_TOPK)
        top_s, pos = lax.top_k(cand, PEER_TOPK)
        idx = jnp.take_along_axis(cidx, pos, axis=-1)
        g = jax.nn.softmax(top_s, axis=-1)
        act = jax.nn.gelu(jnp.einsum('td,thkd->thk', xb, u_tab[idx]).astype(jnp.float32))
        w = (g * act).astype(xb.dtype)
        return jnp.einsum('thk,thkd->td', w, v_tab[idx])

    out = lax.map(block, xp.reshape(Tp // blk, blk, D_MODEL))
    return out.reshape(Tp, D_MODEL)[:T]


def trunk_layer(h, p, lw, attend, h0_re, h0_im):
    bsz, L = h.shape[0], h.shape[1]
    f32 = jnp.float32
    hn = rms_norm(h, lw['g_mix'])
    z = hn @ lw['w_in']
    q, k, v, f, u, ga, gs = jnp.split(z, IN_SPLITS, axis=-1)
    q = q.reshape(bsz, L, ATT_HEADS, HEAD_DIM)
    k = k.reshape(bsz, L, ATT_HEADS, HEAD_DIM)
    v = v.reshape(bsz, L, ATT_HEADS, HEAD_DIM)
    logf = jax.nn.log_sigmoid((f + lw['b_forget']).astype(f32))
    attn = attend(q, k, v, logf)
    u32 = u.astype(f32)
    y, hr, hi = ssm_scan(u32.reshape(bsz, L, SSM_GROUPS, SSM_GROUP),
                         h0_re.astype(f32), h0_im.astype(f32), *lw['ssm'])
    y = y.reshape(bsz, L, SSM_WIDTH) + lw['ssm_d'].astype(f32) * u32
    zs = jax.nn.gelu(y).astype(h.dtype)
    ssm_branch = (zs @ lw['w_glu_a']) * jax.nn.sigmoid(zs @ lw['w_glu_b'])
    attn_branch = attn @ lw['w_attn_out']
    merged = jax.nn.sigmoid(ga) * attn_branch + jax.nn.sigmoid(gs) * ssm_branch
    h = h + merged @ lw['w_out']
    hn = rms_norm(h, lw['g_ffn'])
    h = h + peer(hn.reshape(-1, D_MODEL), lw['peer_w_q'], lw['peer_keys'],
                 lw['peer_u'], lw['peer_v']).reshape(h.shape)
    gate = jax.nn.sigmoid(rms_norm(h, lw['g_ple']) @ lw['w_ple_gate'])
    h = h + (p @ lw['w_ple']) * gate
    return h, (k, v, logf.astype(h.dtype), hr.astype(h.dtype), hi.astype(h.dtype))


def setup_inputs(seed: int = 0) -> dict:
    key = jax.random.key(seed)
    ks = iter(jax.random.split(key, 48))
    f32 = jnp.float32

    def nrm(shape, scale):
        return jax.random.normal(next(ks), shape, f32) * scale

    n_pages = PAST_LEN // PAGE_SIZE
    n_used = DEC_BATCH * n_pages
    n_phys = n_used + max(1, n_used // 4)
    x_prompt = nrm((BATCH, SEQ, D_MODEL), 1.0)
    x_sample = nrm((DEC_BATCH, DEC_SEQ, D_MODEL), 1.0)
    cache_k = nrm((n_phys, DEPTH, PAGE_SIZE, ATT_HEADS, HEAD_DIM), 1.0)
    cache_v = nrm((n_phys, DEPTH, PAGE_SIZE, ATT_HEADS, HEAD_DIM), 1.0)
    cache_logf = jax.nn.log_sigmoid(nrm((n_phys, DEPTH, PAGE_SIZE, ATT_HEADS), 1.0) + 4.0)
    state_ssm_re = nrm((DEC_BATCH, DEPTH, SSM_GROUPS, SSM_STATE), 0.1)
    state_ssm_im = nrm((DEC_BATCH, DEPTH, SSM_GROUPS, SSM_STATE), 0.1)
    page_table = jax.random.permutation(next(ks), n_phys)[:n_used].reshape(DEC_BATCH, n_pages).astype(jnp.int32)
    p_prompt = nrm((DEPTH, BATCH, SEQ, PLE_DIM), 1.0)
    p_sample = nrm((DEPTH, DEC_BATCH, DEC_SEQ, PLE_DIM), 1.0)
    w_in = nrm((DEPTH, D_MODEL, IN_WIDTH), D_MODEL ** -0.5)
    b_forget = jnp.broadcast_to(jnp.linspace(1.0, 6.0, ATT_HEADS, dtype=f32), (DEPTH, ATT_HEADS)) + nrm((DEPTH, ATT_HEADS), 0.1)
    w_attn_out = nrm((DEPTH, ATT_WIDTH, D_MODEL), ATT_WIDTH ** -0.5)
    ssm_lam_re = -0.5 + nrm((DEPTH, SSM_GROUPS, SSM_STATE), 0.01)
    ssm_lam_im = jnp.pi * jnp.arange(SSM_STATE, dtype=f32) + nrm((DEPTH, SSM_GROUPS, SSM_STATE), 0.01)
    ssm_log_dt = jax.random.uniform(next(ks), (DEPTH, SSM_GROUPS), f32, math.log(DT_MIN), math.log(DT_MAX))
    ssm_b_re = nrm((DEPTH, SSM_GROUPS, SSM_STATE, SSM_GROUP), (2 * SSM_GROUP) ** -0.5)
    ssm_b_im = nrm((DEPTH, SSM_GROUPS, SSM_STATE, SSM_GROUP), (2 * SSM_GROUP) ** -0.5)
    ssm_c_re = nrm((DEPTH, SSM_GROUPS, SSM_GROUP, SSM_STATE), SSM_STATE ** -0.5)
    ssm_c_im = nrm((DEPTH, SSM_GROUPS, SSM_GROUP, SSM_STATE), SSM_STATE ** -0.5)
    ssm_d = nrm((DEPTH, SSM_WIDTH), 1.0)
    w_glu_a = nrm((DEPTH, SSM_WIDTH, D_MODEL), SSM_WIDTH ** -0.5)
    w_glu_b = nrm((DEPTH, SSM_WIDTH, D_MODEL), SSM_WIDTH ** -0.5)
    w_out = nrm((DEPTH, D_MODEL, D_MODEL), D_MODEL ** -0.5)
    peer_w_q = nrm((DEPTH, D_MODEL, PEER_HEADS * PEER_KEY_DIM), D_MODEL ** -0.5)
    peer_keys = nrm((DEPTH, PEER_HEADS, 2, N_KEYS, PEER_HALF), PEER_HALF ** -0.5)
    peer_u = nrm((DEPTH, N_EXPERTS, D_MODEL), D_MODEL ** -0.5)
    peer_v = nrm((DEPTH, N_EXPERTS, D_MODEL), PEER_HEADS ** -0.5)
    w_ple = nrm((DEPTH, PLE_DIM, D_MODEL), PLE_DIM ** -0.5)
    w_ple_gate = nrm((DEPTH, D_MODEL, D_MODEL), D_MODEL ** -0.5)
    g_mix = 1.0 + nrm((DEPTH, D_MODEL), 0.05)
    g_ffn = 1.0 + nrm((DEPTH, D_MODEL), 0.05)
    g_ple = 1.0 + nrm((DEPTH, D_MODEL), 0.05)
    g_final = 1.0 + nrm((D_MODEL,), 0.05)
    return {'x_prompt': x_prompt, 'x_sample': x_sample, 'cache_k': cache_k, 'cache_v': cache_v,
            'cache_logf': cache_logf, 'state_ssm_re': state_ssm_re, 'state_ssm_im': state_ssm_im,
            'page_table': page_table, 'p_prompt': p_prompt, 'p_sample': p_sample,
            'w_in': w_in, 'b_forget': b_forget, 'w_attn_out': w_attn_out,
            'ssm_lam_re': ssm_lam_re, 'ssm_lam_im': ssm_lam_im, 'ssm_log_dt': ssm_log_dt,
            'ssm_b_re': ssm_b_re, 'ssm_b_im': ssm_b_im, 'ssm_c_re': ssm_c_re, 'ssm_c_im': ssm_c_im,
            'ssm_d': ssm_d, 'w_glu_a': w_glu_a, 'w_glu_b': w_glu_b, 'w_out': w_out,
            'peer_w_q': peer_w_q, 'peer_keys': peer_keys, 'peer_u': peer_u, 'peer_v': peer_v,
            'w_ple': w_ple, 'w_ple_gate': w_ple_gate,
            'g_mix': g_mix, 'g_ffn': g_ffn, 'g_ple': g_ple, 'g_final': g_final}


def reference(x_prompt, x_sample, cache_k, cache_v, cache_logf, state_ssm_re, state_ssm_im, page_table,
              p_prompt, p_sample, w_in, b_forget, w_attn_out, ssm_lam_re, ssm_lam_im, ssm_log_dt,
              ssm_b_re, ssm_b_im, ssm_c_re, ssm_c_im, ssm_d, w_glu_a, w_glu_b, w_out,
              peer_w_q, peer_keys, peer_u, peer_v, w_ple, w_ple_gate,
              g_mix, g_ffn, g_ple, g_final):
    bp, bs = x_prompt.shape[0], x_sample.shape[0]
    hp, hs = x_prompt, x_sample
    kp, vp, lp, rp, ip = [], [], [], [], []
    ksl, vsl, lsl, rsl, isl = [], [], [], [], []
    for i in range(DEPTH):
        lw = {'g_mix': g_mix[i], 'w_in': w_in[i], 'b_forget': b_forget[i],
              'ssm': ssm_discretise(ssm_lam_re[i], ssm_lam_im[i], ssm_log_dt[i], ssm_b_re[i], ssm_b_im[i])
                     + (ssm_c_re[i], ssm_c_im[i]),
              'ssm_d': ssm_d[i], 'w_glu_a': w_glu_a[i], 'w_glu_b': w_glu_b[i],
              'w_attn_out': w_attn_out[i], 'w_out': w_out[i], 'g_ffn': g_ffn[i],
              'peer_w_q': peer_w_q[i], 'peer_keys': peer_keys[i], 'peer_u': peer_u[i], 'peer_v': peer_v[i],
              'g_ple': g_ple[i], 'w_ple_gate': w_ple_gate[i], 'w_ple': w_ple[i]}
        h0 = jnp.zeros((bp, SSM_GROUPS, SSM_STATE), jnp.float32)
        hp, (k_, v_, l_, r_, m_) = trunk_layer(hp, p_prompt[i], lw, fox_prompt, h0, h0)
        kp.append(k_); vp.append(v_); lp.append(l_); rp.append(r_); ip.append(m_)
        k_past = cache_k[page_table, i].reshape(bs, -1, ATT_HEADS, HEAD_DIM)
        v_past = cache_v[page_table, i].reshape(bs, -1, ATT_HEADS, HEAD_DIM)
        l_past = cache_logf[page_table, i].reshape(bs, -1, ATT_HEADS)
        attend = functools.partial(fox_sample, k_past=k_past, v_past=v_past, logf_past=l_past)
        hs, (k_, v_, l_, r_, m_) = trunk_layer(hs, p_sample[i], lw, attend,
                                               state_ssm_re[:, i], state_ssm_im[:, i])
        ksl.append(k_); vsl.append(v_); lsl.append(l_); rsl.append(r_); isl.append(m_)
    y_prompt = rms_norm(hp, g_final)
    y_sample = rms_norm(hs, g_final)
    k_prompt = jnp.stack(kp, axis=1)
    v_prompt = jnp.stack(vp, axis=1)
    logf_prompt = jnp.stack(lp, axis=1)
    ssm_re_prompt = jnp.stack(rp, axis=1)
    ssm_im_prompt = jnp.stack(ip, axis=1)
    k_sample = jnp.stack(ksl, axis=1)
    v_sample = jnp.stack(vsl, axis=1)
    logf_sample = jnp.stack(lsl, axis=1)
    ssm_re_sample = jnp.stack(rsl, axis=1)
    ssm_im_sample = jnp.stack(isl, axis=1)
    return (y_prompt, y_sample, k_prompt, v_prompt, logf_prompt, ssm_re_prompt, ssm_im_prompt,
            k_sample, v_sample, logf_sample, ssm_re_sample, ssm_im_sample)
```

```python
import functools
import math

import jax
import jax.numpy as jnp
from jax import lax
from jax.experimental import pallas as pl
from jax.experimental.pallas import tpu as pltpu

F32 = jnp.float32
BF16 = jnp.bfloat16
RMS_EPS = 1e-6
LANES = 128
SUBLANES = 8
PEER_TOPK = 16
VMEM_LIMIT = 56 * 1024 * 1024

_TN = (((1,), (1,)), ((), ()))
_TA = (((0,), (0,)), ((), ()))


def _params(sem, vmem=VMEM_LIMIT):
    return pltpu.CompilerParams(dimension_semantics=sem, vmem_limit_bytes=vmem)


def _rms(x, g):
    return x * lax.rsqrt(jnp.mean(x * x, axis=-1, keepdims=True) + RMS_EPS) * g


def _log_sigmoid(x):
    return jnp.minimum(x, 0.0) - jnp.log1p(jnp.exp(-jnp.abs(x)))


def _dot(a, b):
    return jnp.dot(a, b, preferred_element_type=F32)


def _proj_kernel(xp_ref, xs_ref, g_ref, wqkv_ref, wu_ref, wf_ref, bf_ref,
                 q_ref, kb_ref, vb_ref, u_ref, kp_ref, vp_ref, lfp_ref, ks_ref, vs_ref, lfs_ref,
                 *, aw, nh, scale):
    i = pl.program_id(0)
    x = jnp.where(i == 0, xs_ref[...], xp_ref[...])
    hn = _rms(x, g_ref[...]).astype(BF16)
    qkv = _dot(hn, wqkv_ref[...])
    k = qkv[:, aw:2 * aw]
    v = qkv[:, 2 * aw:]
    q_ref[...] = (qkv[:, :aw] * scale).astype(BF16)
    kb_ref[...] = k.astype(BF16)
    vb_ref[...] = v.astype(BF16)
    u_ref[...] = _dot(hn, wu_ref[...])
    lf = _log_sigmoid(_dot(hn, wf_ref[...])[:, :nh] + bf_ref[...])
    kp_ref[...] = k
    vp_ref[...] = v
    lfp_ref[...] = lf

    @pl.when(i == 0)
    def _():
        ks_ref[...] = k
        vs_ref[...] = v
        lfs_ref[...] = lf


def _attn_kernel(q_ref, k_ref, v_ref, lf_ref, o_ref, c_scr, *, tq, hd):
    pr = pl.program_id(1)
    i = pl.program_id(2)
    seq = k_ref.shape[0]
    nkb = seq // tq

    @pl.when(i == 0)
    def _():
        c = lf_ref[0]
        lane = lax.broadcasted_iota(jnp.int32, c.shape, 1)
        sft = 1
        while sft < seq:
            c = c + jnp.where(lane >= sft, pltpu.roll(c, sft, 1), 0.0)
            sft *= 2
        for jj in range(nkb):
            c_scr[jj] = c[:, jj * tq:(jj + 1) * tq]

    q = q_ref[...]
    lane_q = lax.broadcasted_iota(jnp.int32, q.shape, 1)
    row = lax.broadcasted_iota(jnp.int32, (tq, tq), 0)
    col = lax.broadcasted_iota(jnp.int32, (tq, tq), 1)
    outs = []
    for hh in range(LANES // hd):
        qm = jnp.where(lane_q // hd == hh, q, jnp.zeros_like(q))
        head = pr * (LANES // hd) + hh
        c_diag = c_scr[i, pl.ds(head, 1), :]
        c_end = c_diag[:, tq - 1:tq]

        def scores(j):
            off = pl.multiple_of(j * tq, tq)
            s = lax.dot_general(qm, k_ref[pl.ds(off, tq), :], _TN, preferred_element_type=F32)
            return s + (c_end - c_scr[j, pl.ds(head, 1), :]), off

        def update(carry, s, off):
            m, l, acc = carry
            m_new = jnp.maximum(m, jnp.max(s, axis=1, keepdims=True))
            p = jnp.exp(s - m_new)
            alpha = jnp.exp(m - m_new)
            l = alpha * l + jnp.sum(p, axis=1, keepdims=True)
            acc = alpha * acc + _dot(p.astype(BF16), v_ref[pl.ds(off, tq), :])
            return m_new, l, acc

        def body(j, carry):
            s, off = scores(j)
            return update(carry, s, off)

        init = (jnp.full((tq, 1), -jnp.inf, F32), jnp.zeros((tq, 1), F32), jnp.zeros((tq, LANES), F32))
        carry = lax.fori_loop(0, i, body, init)
        s, off = scores(i)
        s = jnp.where(col <= row, s, -jnp.inf)
        _, l, acc = update(carry, s, off)
        outs.append(acc / l)
    out = outs[0]
    for hh in range(1, len(outs)):
        out = jnp.where(lane_q // hd == hh, outs[hh], out)
    o_ref[...] = out.astype(o_ref.dtype)


def _decode_kernel(pt_ref, q_ref, kn_ref, vn_ref, lfn_ref, *refs, ppg, nh, hd):
    k_refs = refs[:ppg]
    v_refs = refs[ppg:2 * ppg]
    lf_refs = refs[2 * ppg:3 * ppg]
    o_ref = refs[3 * ppg]
    qt_scr, lfw_scr, su_scr, m_scr, l_scr, carry_scr, acc_scr = refs[3 * ppg + 1:]
    b = pl.program_id(0)
    j = pl.program_id(1)
    page = k_refs[0].shape[1]
    aw = k_refs[0].shape[2]
    rows = ppg * page

    @pl.when((b == 0) & (j == 0))
    def _():
        lfw_scr[...] = jnp.zeros_like(lfw_scr)
        r = lax.broadcasted_iota(jnp.int32, su_scr.shape, 0)
        c = lax.broadcasted_iota(jnp.int32, su_scr.shape, 1)
        su_scr[...] = jnp.where(c > r, 1.0, 0.0).astype(BF16)

    @pl.when(j == 0)
    def _():
        qrow = q_ref[0]
        r = lax.broadcasted_iota(jnp.int32, (aw, LANES), 0)
        c = lax.broadcasted_iota(jnp.int32, (aw, LANES), 1)
        ones_row = jnp.where(lax.broadcasted_iota(jnp.int32, (page, aw), 0) == 0, 1.0, 0.0)
        q_rows = ones_row.astype(BF16) * qrow
        q_col = lax.dot_general(q_rows, jnp.where(
            lax.broadcasted_iota(jnp.int32, (page, LANES), 0) == 0, 1.0, 0.0).astype(BF16),
            _TA, preferred_element_type=F32)
        qt_scr[...] = jnp.where(r // hd == c, q_col, 0.0).astype(BF16)
        first = lax.broadcasted_iota(jnp.int32, (page, aw), 0) == 0
        k_ext = jnp.where(first, kn_ref[0], 0.0).astype(BF16)
        v_ext = jnp.where(first, vn_ref[0], 0.0).astype(BF16)
        s_new = _dot(k_ext, qt_scr[...])
        m_scr[...] = s_new[0:1, :]
        l_scr[...] = jnp.ones_like(l_scr)
        p_ext = jnp.where(lax.broadcasted_iota(jnp.int32, (page, LANES), 0) == 0, 1.0, 0.0).astype(BF16)
        acc_scr[...] = lax.dot_general(v_ext, p_ext, _TA, preferred_element_type=F32)
        lfn = lfn_ref[0]
        carry_scr[...] = jnp.zeros_like(carry_scr)
        carry_scr[:, 0:nh] = lfn

    for r in range(ppg):
        lfw_scr[r * page:(r + 1) * page, 0:nh] = lf_refs[r][0]
    lfw = lfw_scr[...]
    hi = lfw.astype(BF16)
    r1 = lfw - hi.astype(F32)
    mid = r1.astype(BF16)
    lo = (r1 - mid.astype(F32)).astype(BF16)
    su = su_scr[...]
    later = _dot(su, hi) + _dot(su, mid) + _dot(su, lo)
    carry = carry_scr[...]
    bias = later + carry
    carry_scr[...] = carry + jnp.sum(lfw, axis=0, keepdims=True)

    kb = jnp.concatenate([k_refs[r][0].astype(BF16) for r in range(ppg)], axis=0)
    vb = jnp.concatenate([v_refs[r][0].astype(BF16) for r in range(ppg)], axis=0)
    s = _dot(kb, qt_scr[...]) + bias
    m = m_scr[...]
    m_new = jnp.maximum(m, jnp.max(s, axis=0, keepdims=True))
    p = jnp.exp(s - m_new)
    alpha = jnp.exp(m - m_new)
    l_scr[...] = alpha * l_scr[...] + jnp.sum(p, axis=0, keepdims=True)
    m_scr[...] = m_new
    pv = lax.dot_general(vb, p.astype(BF16), _TA, preferred_element_type=F32)
    acc_scr[...] = alpha * acc_scr[...] + pv

    @pl.when(j == pl.num_programs(1) - 1)
    def _():
        out = acc_scr[...] / l_scr[...]
        r = lax.broadcasted_iota(jnp.int32, out.shape, 0)
        c = lax.broadcasted_iota(jnp.int32, out.shape, 1)
        o_ref[0] = jnp.sum(jnp.where(r // hd == c, out, 0.0), axis=1, keepdims=True)


def _scan_kernel(u_ref, h0r_ref, h0i_ref, ar_ref, ai_ref, bsub_ref, cre_ref, cim_ref,
                 y_ref, hr_ref, hi_ref, bu_scr, xr_scr, xi_scr, sr_scr, si_scr, *, nb, lc):
    c = pl.program_id(0)

    @pl.when(c == 0)
    def _():
        sr_scr[...] = h0r_ref[...]
        si_scr[...] = h0i_ref[...]

    nch = bsub_ref.shape[0]
    sn = ar_ref.shape[1]
    w = sn // nch
    cw = u_ref.shape[1] // nch
    u = u_ref[...].astype(BF16)
    for cc in range(nch):
        r = _dot(u[:, cc * cw:(cc + 1) * cw], bsub_ref[cc])
        bu_scr[:, cc * w:(cc + 1) * w] = r[:, :w]
        bu_scr[:, sn + cc * w:sn + (cc + 1) * w] = r[:, w:]
    ar = jnp.broadcast_to(ar_ref[...], (nb, sn))
    ai = jnp.broadcast_to(ai_ref[...], (nb, sn))

    def step(t, carry):
        xr, xi = carry
        off = pl.multiple_of(t * nb, nb)
        nxr = ar * xr - ai * xi + bu_scr[pl.ds(off, nb), 0:sn]
        nxi = ar * xi + ai * xr + bu_scr[pl.ds(off, nb), sn:2 * sn]
        xr_scr[pl.ds(off, nb), :] = nxr
        xi_scr[pl.ds(off, nb), :] = nxi
        return nxr, nxi

    xr, xi = lax.fori_loop(0, lc, step, (sr_scr[...], si_scr[...]))
    sr_scr[...] = xr
    si_scr[...] = xi
    for cc in range(nch):
        y_ref[:, cc * cw:(cc + 1) * cw] = (
            _dot(xr_scr[:, cc * w:(cc + 1) * w].astype(BF16), cre_ref[cc])
            - _dot(xi_scr[:, cc * w:(cc + 1) * w].astype(BF16), cim_ref[cc]))

    @pl.when(c == pl.num_programs(0) - 1)
    def _():
        hr_ref[...] = xr
        hi_ref[...] = xi


def _mix_kernel(xp_ref, xs_ref, ap_ref, as_ref, yp_ref, ys_ref, u_ref, gmix_ref, gffn_ref, d_ref,
                wg_ref, wa_ref, wb_ref, wao_ref, wo_ref, h1_ref, hn2_ref, *, dm):
    i = pl.program_id(0)
    smp = i == 0
    x = jnp.where(smp, xs_ref[...], xp_ref[...])
    attn = jnp.where(smp, as_ref[...], ap_ref[...])
    y = jnp.where(smp, ys_ref[...], yp_ref[...])
    hn = _rms(x, gmix_ref[...]).astype(BF16)
    g = _dot(hn, wg_ref[...])
    zs = jax.nn.gelu(y + d_ref[...] * u_ref[...]).astype(BF16)
    ssm_branch = _dot(zs, wa_ref[...]) * jax.nn.sigmoid(_dot(zs, wb_ref[...]))
    attn_branch = _dot(attn, wao_ref[...])
    merged = jax.nn.sigmoid(g[:, :dm]) * attn_branch + jax.nn.sigmoid(g[:, dm:]) * ssm_branch
    h1 = x + _dot(merged.astype(BF16), wo_ref[...])
    h1_ref[...] = h1
    hn2_ref[...] = _rms(h1, gffn_ref[...]).astype(BF16)


def _peer_cells():
    return [(a, b) for a in range(PEER_TOPK) for b in range(PEER_TOPK) if (a + 1) * (b + 1) <= PEER_TOPK]


def _top_ranks(s, sub):
    n = s.shape[0]
    work = s
    rank = jnp.full(s.shape, float(PEER_TOPK), F32)
    vals = []
    for r in range(PEER_TOPK):
        mx = jnp.max(work, axis=0, keepdims=True)
        idx = jnp.where(work == mx, sub, float(n))
        first = idx == jnp.min(idx, axis=0, keepdims=True)
        rank = jnp.where(first, float(r), rank)
        work = jnp.where(first, -jnp.inf, work)
        vals.append(mx)
    return rank, vals


def _peer_select_kernel(x_ref, wq_ref, keys_ref, cell_ref, n_ref, e1_ref, r2_ref, e2_ref, st_scr, *, nhp):
    tb = x_ref.shape[0]
    ntg = tb // LANES
    nk = keys_ref.shape[1]
    half = keys_ref.shape[2]
    qp = _dot(x_ref[...], wq_ref[...]).astype(BF16)
    for hp in range(2 * nhp):
        st = lax.dot_general(keys_ref[hp], qp[:, hp * half:(hp + 1) * half], _TN,
                             preferred_element_type=F32)
        for tg in range(ntg):
            st_scr[hp, tg] = st[:, tg * LANES:(tg + 1) * LANES]

    cells = _peer_cells()
    ncell = cell_ref.shape[0]
    starts = {}
    for pos, (a, b) in enumerate(cells):
        starts.setdefault(a, [pos, 0])
        starts[a][1] += 1

    def body(it, _):
        h = it // ntg
        tg = it % ntg
        sub = lax.broadcasted_iota(jnp.int32, (nk, LANES), 0).astype(F32)
        s1 = st_scr[2 * h, tg]
        s2 = st_scr[2 * h + 1, tg]
        rank1, v1 = _top_ranks(s1, sub)
        rank2, v2 = _top_ranks(s2, sub)
        rows = [v1[a] + v2[b] for a, b in cells]
        rows += [jnp.full((1, LANES), -jnp.inf, F32)] * (ncell - len(cells))
        cand = jnp.concatenate(rows, axis=0)
        flat = cell_ref[...]
        work = cand
        sel = jnp.zeros(cand.shape, F32)
        for _r in range(PEER_TOPK):
            mx = jnp.max(work, axis=0, keepdims=True)
            idx = jnp.where(work == mx, flat, float(PEER_TOPK * PEER_TOPK))
            first = idx == jnp.min(idx, axis=0, keepdims=True)
            sel = jnp.where(first, 1.0, sel)
            work = jnp.where(first, -jnp.inf, work)
        top = v1[0] + v2[0]
        z = jnp.sum(sel * jnp.exp(jnp.where(sel > 0.0, cand - top, 0.0)), axis=0, keepdims=True)
        n_keys = jnp.zeros((nk, LANES), F32)
        for a, (st0, cnt) in starts.items():
            n_a = jnp.sum(sel[st0:st0 + cnt], axis=0, keepdims=True)
            n_keys = jnp.where(rank1 == float(a), n_a, n_keys)
        n_ref[h, tg] = n_keys
        r2_ref[h, tg] = rank2
        e1_ref[h, tg] = jnp.exp(s1 - v1[0])
        e2_ref[h, tg] = jnp.exp(s2 - v2[0]) / z
        return 0

    lax.fori_loop(0, nhp * ntg, body, 0)


def _peer_kernel(x_ref, u_ref, vt_ref, n_ref, e1_ref, r2_ref, e2_ref, o_ref, acc_scr, act_scr, g_scr, *, nhp):
    j = pl.program_id(1)
    tb = x_ref.shape[0]
    ntg = tb // LANES
    nk = r2_ref.shape[2]
    na = n_ref.shape[2]

    @pl.when(j == 0)
    def _():
        acc_scr[...] = jnp.zeros_like(acc_scr)

    act_scr[...] = lax.dot_general(u_ref[...], x_ref[...], _TN, preferred_element_type=F32)

    def body(a, _):
        off = pl.multiple_of(a * nk, nk)
        for tg in range(ntg):
            gate = jnp.zeros((nk, LANES), F32)
            for h in range(nhp):
                n_row = n_ref[h, tg, pl.ds(a, 1), :]
                e1_row = e1_ref[h, tg, pl.ds(a, 1), :]
                gate = gate + jnp.where(r2_ref[h, tg] < n_row, e1_row * e2_ref[h, tg], 0.0)
            act = act_scr[pl.ds(off, nk), tg * LANES:(tg + 1) * LANES]
            g_scr[pl.ds(off, nk), tg * LANES:(tg + 1) * LANES] = (gate * jax.nn.gelu(act)).astype(BF16)
        return 0

    lax.fori_loop(0, na, body, 0)
    acc_scr[...] += _dot(vt_ref[...], g_scr[...])

    @pl.when(j == pl.num_programs(1) - 1)
    def _():
        o_ref[...] = acc_scr[...].T


def _ple_kernel(h1_ref, po_ref, pp_ref, ps_ref, gple_ref, gfin_ref, wpg_ref, wp_ref, yp_ref, ys_ref):
    i = pl.program_id(0)
    p = jnp.where(i == 0, ps_ref[...], pp_ref[...]).astype(BF16)
    h2 = h1_ref[...] + po_ref[...]
    gate = jax.nn.sigmoid(_dot(_rms(h2, gple_ref[...]).astype(BF16), wpg_ref[...]))
    h3 = h2 + _dot(p, wp_ref[...]) * gate
    y = _rms(h3, gfin_ref[...])
    yp_ref[...] = y

    @pl.when(i == 0)
    def _():
        ys_ref[...] = y


def _ssm_discretise(lam_re, lam_im, log_dt, b_re, b_im):
    dt = jnp.exp(log_dt)[:, None]
    mag = jnp.exp(lam_re * dt)
    ang = lam_im * dt
    a_re, a_im = mag * jnp.cos(ang), mag * jnp.sin(ang)
    den = lam_re * lam_re + lam_im * lam_im
    n_re, n_im = a_re - 1.0, a_im
    c_re = (n_re * lam_re + n_im * lam_im) / den
    c_im = (n_im * lam_re - n_re * lam_im) / den
    bb_re = c_re[..., None] * b_re - c_im[..., None] * b_im
    bb_im = c_re[..., None] * b_im + c_im[..., None] * b_re
    return a_re, a_im, bb_re, bb_im


def _pad_rows(a, rows):
    return jnp.pad(a, ((0, rows - a.shape[0]),) + ((0, 0),) * (a.ndim - 1))


def _const(shape):
    nd = len(shape)
    return pl.BlockSpec(shape, lambda *_: (0,) * nd)


def kernel(x_prompt, x_sample, cache_k, cache_v, cache_logf, state_ssm_re, state_ssm_im, page_table, p_prompt, p_sample, w_in, b_forget, w_attn_out, ssm_lam_re, ssm_lam_im, ssm_log_dt, ssm_b_re, ssm_b_im, ssm_c_re, ssm_c_im, ssm_d, w_glu_a, w_glu_b, w_out, peer_w_q, peer_keys, peer_u, peer_v, w_ple, w_ple_gate, g_mix, g_ffn, g_ple, g_final):
    nb, seq, dm = x_prompt.shape
    nbs, dec_seq, _ = x_sample.shape
    depth = w_in.shape[0]
    assert depth == 1 and dec_seq == 1
    nh = b_forget.shape[-1]
    aw = w_attn_out.shape[1]
    hd = aw // nh
    ng, ns, gc = ssm_b_re.shape[1:]
    sw = ng * gc
    sn = ng * ns
    nhp, _, nk, half = peer_keys.shape[1:]
    n_exp = peer_u.shape[1]
    pdim = w_ple.shape[1]
    page = cache_k.shape[2]
    n_pages = page_table.shape[1]
    assert hd * 2 == LANES and nk == LANES and half == LANES and nb % SUBLANES == 0 and nbs % SUBLANES == 0
    scale = hd ** -0.5
    assert math.log2(scale) == int(math.log2(scale))

    tm = min(512, seq)
    nlb = seq // tm
    t_p = nb * seq
    n_pb = t_p // tm
    n_blk = n_pb + 1
    t_all = n_blk * tm
    assert seq % tm == 0 and nbs <= tm

    def prompt_blk(i):
        return jnp.maximum(i - 1, 0)

    def uni_blk(i):
        return (i + n_pb) % n_blk

    def tb_blk(i):
        pb = prompt_blk(i)
        return jnp.where(i == 0, nlb, pb % nlb), jnp.where(i == 0, 0, pb // nlb)

    xp = x_prompt.reshape(t_p, dm)
    xs = _pad_rows(x_sample.reshape(nbs, dm), tm)

    w_in0 = w_in[0]
    wqkv = w_in0[:, :3 * aw].astype(BF16)
    wf = jnp.pad(w_in0[:, 3 * aw:3 * aw + nh], ((0, 0), (0, LANES - nh))).astype(BF16)
    wu = w_in0[:, 3 * aw + nh:3 * aw + nh + sw].astype(BF16)
    wg = w_in0[:, 3 * aw + nh + sw:].astype(BF16)
    a_re, a_im, bb_re, bb_im = _ssm_discretise(ssm_lam_re[0], ssm_lam_im[0], ssm_log_dt[0], ssm_b_re[0], ssm_b_im[0])
    gpc = LANES // gc
    nch = ng // gpc
    eye = jnp.eye(gpc, dtype=F32)

    def blockdiag_in(bb):
        return jnp.einsum('kgnc,gh->kgchn', bb.reshape(nch, gpc, ns, gc), eye).reshape(nch, gpc * gc, gpc * ns)

    def blockdiag_out(cc):
        return jnp.einsum('kgcn,gh->kgnhc', cc.reshape(nch, gpc, gc, ns), eye).reshape(nch, gpc * ns, gpc * gc)

    bsub = jnp.concatenate([blockdiag_in(bb_re), blockdiag_in(bb_im)], axis=2).astype(BF16)
    cre = blockdiag_out(ssm_c_re[0]).astype(BF16)
    cim = blockdiag_out(ssm_c_im[0]).astype(BF16)
    a_re = a_re.reshape(1, sn)
    a_im = a_im.reshape(1, sn)

    row = lambda w: pl.BlockSpec((tm, w), lambda i: (prompt_blk(i), 0))
    uni = lambda w: pl.BlockSpec((tm, w), lambda i: (uni_blk(i), 0))
    smp = lambda w: pl.BlockSpec((tm, w), lambda i: (0, 0))
    q_u, kb_u, vb_u, u_tb, k_p, v_p, lf_p, k_s, v_s, lf_s = pl.pallas_call(
        functools.partial(_proj_kernel, aw=aw, nh=nh, scale=scale),
        grid=(n_blk,),
        in_specs=[row(dm), smp(dm), _const((1, dm)), _const((dm, 3 * aw)), _const((dm, sw)),
                  _const((dm, LANES)), _const((1, nh))],
        out_specs=[uni(aw), uni(aw), uni(aw), pl.BlockSpec((tm, sw), tb_blk),
                   row(aw), row(aw), row(nh), smp(aw), smp(aw), smp(nh)],
        out_shape=[jax.ShapeDtypeStruct((t_all, aw), BF16)] * 3
        + [jax.ShapeDtypeStruct((seq + tm, nb * sw), F32)]
        + [jax.ShapeDtypeStruct((t_p, aw), F32)] * 2 + [jax.ShapeDtypeStruct((t_p, nh), F32)]
        + [jax.ShapeDtypeStruct((tm, aw), F32)] * 2 + [jax.ShapeDtypeStruct((tm, nh), F32)],
        compiler_params=_params(("arbitrary",)),
        name="proj",
    )(xp, xs, g_mix, wqkv, wu, wf, b_forget)

    tq = min(256, seq)
    nq = seq // tq
    lf_t = lf_p.reshape(nb, seq, nh).transpose(0, 2, 1)
    attn_u = pl.pallas_call(
        functools.partial(_attn_kernel, tq=tq, hd=hd),
        grid=(nb, aw // LANES, nq),
        in_specs=[pl.BlockSpec((tq, LANES), lambda b, p, i: (b * nq + i, p)),
                  pl.BlockSpec((seq, LANES), lambda b, p, i: (b, p)),
                  pl.BlockSpec((seq, LANES), lambda b, p, i: (b, p)),
                  pl.BlockSpec((1, nh, seq), lambda b, p, i: (b, 0, 0))],
        out_specs=pl.BlockSpec((tq, LANES), lambda b, p, i: (b * nq + i, p)),
        out_shape=jax.ShapeDtypeStruct((t_all, aw), BF16),
        scratch_shapes=[pltpu.VMEM((nq, nh, tq), F32)],
        compiler_params=_params(("arbitrary", "arbitrary", "arbitrary")),
        name="fox_prompt",
    )(q_u, kb_u, vb_u, lf_t)

    ppg = min(8, n_pages)
    n_grp = n_pages // ppg
    assert n_pages % ppg == 0
    ck = cache_k.reshape(cache_k.shape[0], page, aw)
    cv = cache_v.reshape(cache_v.shape[0], page, aw)
    cl = cache_logf.reshape(cache_logf.shape[0], page, nh)

    def page_spec(r, w):
        return pl.BlockSpec((1, page, w), lambda b, j, pt: (pt[b, (n_grp - 1 - j) * ppg + r], 0, 0))

    tok = lambda w: pl.BlockSpec((1, 1, w), lambda b, j, pt: (b, 0, 0))
    attn_s = pl.pallas_call(
        functools.partial(_decode_kernel, ppg=ppg, nh=nh, hd=hd),
        grid_spec=pltpu.PrefetchScalarGridSpec(
            num_scalar_prefetch=1,
            grid=(nbs, n_grp),
            in_specs=[tok(aw), tok(aw), tok(aw), tok(nh)]
            + [page_spec(r, aw) for r in range(ppg)] * 2 + [page_spec(r, nh) for r in range(ppg)],
            out_specs=pl.BlockSpec((1, aw, 1), lambda b, j, pt: (b, 0, 0)),
            scratch_shapes=[pltpu.VMEM((aw, LANES), BF16), pltpu.VMEM((ppg * page, LANES), F32),
                            pltpu.VMEM((ppg * page, ppg * page), BF16), pltpu.VMEM((1, LANES), F32),
                            pltpu.VMEM((1, LANES), F32), pltpu.VMEM((1, LANES), F32),
                            pltpu.VMEM((aw, LANES), F32)]),
        out_shape=jax.ShapeDtypeStruct((nbs, aw, 1), F32),
        compiler_params=_params(("arbitrary", "arbitrary")),
        name="fox_decode",
    )(page_table, q_u[t_p:t_p + nbs].reshape(nbs, 1, aw), k_s[:nbs].reshape(nbs, 1, aw),
      v_s[:nbs].reshape(nbs, 1, aw), lf_s[:nbs].reshape(nbs, 1, nh),
      *([ck] * ppg), *([cv] * ppg), *([cl] * ppg))
    attn_s = _pad_rows(attn_s.reshape(nbs, aw).astype(BF16), tm)

    def scan(u_rows, h0r, h0i, nbatch, steps, lc):
        n_chunks = steps // lc
        rows = lc * nbatch
        return pl.pallas_call(
            functools.partial(_scan_kernel, nb=nbatch, lc=lc),
            grid=(n_chunks,),
            in_specs=[pl.BlockSpec((rows, sw), lambda c: (c, 0)), _const((nbatch, sn)), _const((nbatch, sn)),
                      _const((1, sn)), _const((1, sn)), _const(bsub.shape), _const(cre.shape), _const(cim.shape)],
            out_specs=[pl.BlockSpec((rows, sw), lambda c: (c, 0)), _const((nbatch, sn)), _const((nbatch, sn))],
            out_shape=[jax.ShapeDtypeStruct((steps * nbatch, sw), F32)]
            + [jax.ShapeDtypeStruct((nbatch, sn), F32)] * 2,
            scratch_shapes=[pltpu.VMEM((rows, 2 * sn), F32), pltpu.VMEM((rows, sn), F32),
                            pltpu.VMEM((rows, sn), F32), pltpu.VMEM((nbatch, sn), F32),
                            pltpu.VMEM((nbatch, sn), F32)],
            compiler_params=_params(("arbitrary",)),
            name="s5_scan",
        )(u_rows, h0r, h0i, a_re, a_im, bsub, cre, cim)

    lc = min(64, seq)
    zeros_state = jnp.zeros((nb, sn), F32)
    u_rows = u_tb.reshape((seq + tm) * nb, sw)
    y_tb, hr_p, hi_p = scan(u_rows, zeros_state, zeros_state, nb, seq, lc)
    u_smp = u_tb.reshape(seq + tm, nb, sw)[seq:seq + nbs, 0]
    y_s, hr_s, hi_s = scan(u_smp, state_ssm_re[:, 0].reshape(nbs, sn), state_ssm_im[:, 0].reshape(nbs, sn),
                           nbs, 1, 1)
    y_s = _pad_rows(y_s, tm)
    y_tb = y_tb.reshape(seq, nb * sw)

    tbp = pl.BlockSpec((tm, sw), lambda i: (prompt_blk(i) % nlb, prompt_blk(i) // nlb))
    h1_u, hn2_u = pl.pallas_call(
        functools.partial(_mix_kernel, dm=dm),
        grid=(n_blk,),
        in_specs=[row(dm), smp(dm), uni(aw), smp(aw), tbp, smp(sw), pl.BlockSpec((tm, sw), tb_blk),
                  _const((1, dm)), _const((1, dm)), _const((1, sw)),
                  _const((dm, 2 * dm)), _const((sw, dm)), _const((sw, dm)), _const((aw, dm)), _const((dm, dm))],
        out_specs=[uni(dm), uni(dm)],
        out_shape=[jax.ShapeDtypeStruct((t_all, dm), F32), jax.ShapeDtypeStruct((t_all, dm), BF16)],
        compiler_params=_params(("arbitrary",)),
        name="mix",
    )(xp, xs, attn_u, attn_s, y_tb, y_s, u_tb, g_mix, g_ffn, ssm_d,
      wg, w_glu_a[0].astype(BF16), w_glu_b[0].astype(BF16), w_attn_out[0].astype(BF16), w_out[0].astype(BF16))

    tb = tm
    ntg = tb // LANES
    n_tg = t_all // LANES
    cells = _peer_cells()
    ncell = -(-len(cells) // SUBLANES) * SUBLANES
    cell_idx = jnp.asarray([float(a * PEER_TOPK + b) for a, b in cells]
                           + [float(PEER_TOPK * PEER_TOPK)] * (ncell - len(cells)), F32)
    cell_idx = jnp.broadcast_to(cell_idx[:, None], (ncell, LANES))
    keys_b = peer_keys[0].reshape(nhp * 2, nk, half).astype(BF16)
    tab = jax.ShapeDtypeStruct((nhp, n_tg, nk, LANES), F32)
    tab_spec = pl.BlockSpec((nhp, ntg, nk, LANES), lambda i: (0, i, 0, 0))
    n_tab, e1_tab, r2_tab, e2_tab = pl.pallas_call(
        functools.partial(_peer_select_kernel, nhp=nhp),
        grid=(n_blk,),
        in_specs=[pl.BlockSpec((tb, dm), lambda i: (i, 0)), _const((dm, nhp * 2 * half)),
                  _const((nhp * 2, nk, half)), _const((ncell, LANES))],
        out_specs=[tab_spec] * 4,
        out_shape=[tab] * 4,
        scratch_shapes=[pltpu.VMEM((nhp * 2, ntg, nk, LANES), F32)],
        compiler_params=_params(("arbitrary",)),
        name="peer_select",
    )(hn2_u, peer_w_q[0].astype(BF16), keys_b, cell_idx)

    ec = min(1024, n_exp)
    na = ec // nk
    u_b = peer_u[0].astype(BF16)
    vt_b = peer_v[0].astype(BF16).T
    peer_out = pl.pallas_call(
        functools.partial(_peer_kernel, nhp=nhp),
        grid=(n_blk, n_exp // ec),
        in_specs=[pl.BlockSpec((tb, dm), lambda i, j: (i, 0)),
                  pl.BlockSpec((ec, dm), lambda i, j: (j, 0)),
                  pl.BlockSpec((dm, ec), lambda i, j: (0, j)),
                  pl.BlockSpec((nhp, ntg, na, LANES), lambda i, j: (0, i, j, 0)),
                  pl.BlockSpec((nhp, ntg, na, LANES), lambda i, j: (0, i, j, 0)),
                  pl.BlockSpec((nhp, ntg, nk, LANES), lambda i, j: (0, i, 0, 0)),
                  pl.BlockSpec((nhp, ntg, nk, LANES), lambda i, j: (0, i, 0, 0))],
        out_specs=pl.BlockSpec((tb, dm), lambda i, j: (i, 0)),
        out_shape=jax.ShapeDtypeStruct((t_all, dm), F32),
        scratch_shapes=[pltpu.VMEM((dm, tb), F32), pltpu.VMEM((ec, tb), F32), pltpu.VMEM((ec, tb), BF16)],
        compiler_params=_params(("arbitrary", "arbitrary")),
        name="peer_experts",
    )(hn2_u, u_b, vt_b, n_tab, e1_tab, r2_tab, e2_tab)

    pp = p_prompt[0].reshape(t_p, pdim)
    ps = _pad_rows(p_sample[0].reshape(nbs, pdim), tm)
    y_p, y_s_out = pl.pallas_call(
        _ple_kernel,
        grid=(n_blk,),
        in_specs=[uni(dm), uni(dm), row(pdim), smp(pdim), _const((1, dm)), _const((1, dm)),
                  _const((dm, dm)), _const((pdim, dm))],
        out_specs=[row(dm), smp(dm)],
        out_shape=[jax.ShapeDtypeStruct((t_p, dm), F32), jax.ShapeDtypeStruct((tm, dm), F32)],
        compiler_params=_params(("arbitrary",)),
        name="ple_final",
    )(h1_u, peer_out, pp, ps, g_ple, g_final.reshape(1, dm), w_ple_gate[0].astype(BF16), w_ple[0].astype(BF16))

    return (y_p.reshape(nb, seq, dm), y_s_out[:nbs].reshape(nbs, 1, dm),
            k_p.reshape(nb, 1, seq, nh, hd), v_p.reshape(nb, 1, seq, nh, hd), lf_p.reshape(nb, 1, seq, nh),
            hr_p.reshape(nb, 1, ng, ns), hi_p.reshape(nb, 1, ng, ns),
            k_s[:nbs].reshape(nbs, 1, 1, nh, hd), v_s[:nbs].reshape(nbs, 1, 1, nh, hd),
            lf_s[:nbs].reshape(nbs, 1, 1, nh),
            hr_s.reshape(nbs, 1, ng, ns), hi_s.reshape(nbs, 1, ng, ns))
```
